```python
import jax, jax.numpy as jnp
from jax import lax
import numpy as np

D_MODEL = 1024
BATCH = 8
SEQ = 2048
DEPTH = 2

MIX_WIDTH = D_MODEL // 2
RET_HEADS = 4
RET_V_DIM = MIX_WIDTH // RET_HEADS
RET_QK_DIM = RET_V_DIM // 2
RET_CHUNK = 128
SC_WIDTH = MIX_WIDTH
SC_KERNEL = 3
CF_WIDTH = MIX_WIDTH
CF_KERNEL = 31
N_BRANCH = 3
D_FF = int(round(8 * D_MODEL / 3 / 256)) * 256
FFN_KERNEL = 3
ROPE_BASE = 10000.0
EPS = 1e-6

IN_SIZES = [RET_HEADS * RET_QK_DIM, RET_HEADS * RET_QK_DIM, MIX_WIDTH, MIX_WIDTH,
            SC_WIDTH, SC_WIDTH, SC_WIDTH, CF_WIDTH, CF_WIDTH, N_BRANCH * D_MODEL]
IN_WIDTH = sum(IN_SIZES)

kernel_name = "hybrid_retention_shortconv_conformer_gated_block"


def rmsnorm(x, g):
    xf = x.astype(jnp.float32)
    y = xf * lax.rsqrt(jnp.mean(xf * xf, axis=-1, keepdims=True) + EPS)
    return (y * g.astype(jnp.float32)).astype(x.dtype)


def layernorm(x, g, b):
    xf = x.astype(jnp.float32)
    mu = jnp.mean(xf, axis=-1, keepdims=True)
    xc = xf - mu
    y = xc * lax.rsqrt(jnp.mean(xc * xc, axis=-1, keepdims=True) + EPS)
    return (y * g.astype(jnp.float32) + b.astype(jnp.float32)).astype(x.dtype)


def causal_dwconv(x, w, b=None):
    K, C = w.shape
    xp = jnp.pad(x, ((0, 0), (K - 1, 0), (0, 0)))
    y = lax.conv_general_dilated(xp, w.astype(x.dtype)[:, None, :], window_strides=(1,),
                                 padding='VALID', dimension_numbers=('NWC', 'WIO', 'NWC'),
                                 feature_group_count=C)
    if b is not None:
        y = y + b.astype(x.dtype)
    return y


def rotary(x, pos):
    half = x.shape[-1] // 2
    inv = ROPE_BASE ** (-jnp.arange(half, dtype=jnp.float32) / half)
    ang = pos.astype(jnp.float32)[:, None] * inv[None, :]
    cos = jnp.cos(ang)[None, :, None, :]
    sin = jnp.sin(ang)[None, :, None, :]
    x1, x2 = x[..., :half], x[..., half:]
    return jnp.concatenate([x1 * cos - x2 * sin, x2 * cos + x1 * sin], axis=-1)


def retention_chunkwise(q, k, v):
    B, S, H, dk = q.shape
    dv = v.shape[-1]
    C = RET_CHUNK
    N = S // C
    gamma = 1.0 - jnp.exp2(-5.0 - jnp.arange(H, dtype=jnp.float32))
    log_g = jnp.log(gamma)
    i = jnp.arange(C, dtype=jnp.float32)
    diff = i[:, None] - i[None, :]
    decay = jnp.where(diff[None] >= 0,
                      jnp.exp(jnp.maximum(diff, 0.0)[None] * log_g[:, None, None]), 0.0)
    zeta = jnp.exp((C - 1.0 - i)[None, :] * log_g[:, None])
    xi = jnp.exp((i + 1.0)[None, :] * log_g[:, None])
    chunk_decay = jnp.exp(C * log_g)

    def chunks(t):
        return t.reshape(B, N, C, H, t.shape[-1]).transpose(0, 3, 1, 2, 4)

    qc, kc, vc = chunks(q), chunks(k), chunks(v)
    scores = jnp.einsum('bhncd,bhnmd->bhncm', qc, kc) * decay[None, :, None]
    o_inner = jnp.einsum('bhncm,bhnme->bhnce', scores, vc)
    kv = jnp.einsum('bhnmd,bhnme->bhnde', kc * zeta[None, :, None, :, None], vc)

    def step(R, kv_n):
        return chunk_decay[None, :, None, None] * R + kv_n, R

    _, R_prev = lax.scan(step, jnp.zeros((B, H, dk, dv), jnp.float32), jnp.moveaxis(kv, 2, 0))
    R_prev = jnp.moveaxis(R_prev, 0, 2)
    o_cross = jnp.einsum('bhncd,bhnde->bhnce', qc, R_prev) * xi[None, :, None, :, None]
    return (o_inner + o_cross).transpose(0, 2, 3, 1, 4).reshape(B, S, H, dv)


def head_groupnorm(o):
    mu = jnp.mean(o, axis=-1, keepdims=True)
    oc = o - mu
    return oc * lax.rsqrt(jnp.mean(oc * oc, axis=-1, keepdims=True) + EPS)


def setup_inputs(seed: int = 0) -> dict:
    key = jax.random.key(seed)
    ks = jax.random.split(key, 20)
    f32 = jnp.float32

    def w(k, shape, fan_in):
        return jax.random.normal(k, shape, f32) * (fan_in ** -0.5)

    def gain(k, shape):
        return 1.0 + 0.02 * jax.random.normal(k, shape, f32)

    def small(k, shape):
        return 0.01 * jax.random.normal(k, shape, f32)

    L = DEPTH
    return {
        "x": jax.random.normal(ks[0], (BATCH, SEQ, D_MODEL), f32),
        "norm_mix_pre": gain(ks[1], (L, D_MODEL)),
        "norm_mix_post": gain(ks[2], (L, D_MODEL)),
        "norm_ffn_pre": gain(ks[3], (L, D_MODEL)),
        "norm_ffn_post": gain(ks[4], (L, D_MODEL)),
        "w_in": w(ks[5], (L, D_MODEL, IN_WIDTH), D_MODEL),
        "w_ret_out": w(ks[6], (L, MIX_WIDTH, D_MODEL), MIX_WIDTH),
        "sc_conv_w": w(ks[7], (L, SC_KERNEL, SC_WIDTH), SC_KERNEL),
        "w_sc_out": w(ks[8], (L, SC_WIDTH, D_MODEL), SC_WIDTH),
        "cf_conv_w": w(ks[9], (L, CF_KERNEL, CF_WIDTH), CF_KERNEL),
        "cf_conv_b": small(ks[10], (L, CF_WIDTH)),
        "cf_ln_g": gain(ks[11], (L, CF_WIDTH)),
        "cf_ln_b": small(ks[12], (L, CF_WIDTH)),
        "w_cf_out": w(ks[13], (L, CF_WIDTH, D_MODEL), CF_WIDTH),
        "w_o": w(ks[14], (L, D_MODEL, D_MODEL), D_MODEL),
        "w_up": w(ks[15], (L, D_MODEL, 2 * D_FF), D_MODEL),
        "ffn_conv_w": w(ks[16], (L, FFN_KERNEL, 2 * D_FF), FFN_KERNEL),
        "w_down": w(ks[17], (L, D_FF, D_MODEL), D_FF),
    }


def reference(x, norm_mix_pre, norm_mix_post, norm_ffn_pre, norm_ffn_post, w_in, w_ret_out,
              sc_conv_w, w_sc_out, cf_conv_w, cf_conv_b, cf_ln_g, cf_ln_b, w_cf_out, w_o,
              w_up, ffn_conv_w, w_down):
    B, S, D = x.shape
    dt = x.dtype
    pos = jnp.arange(S)
    split_idx = [int(c) for c in np.cumsum(IN_SIZES)[:-1]]
    for l in range(DEPTH):
        h = rmsnorm(x, norm_mix_pre[l])
        proj = h @ w_in[l].astype(dt)
        (q, k, v, g_ret, sc_b, sc_c, sc_x, cf_a, cf_b, gate_logits) = jnp.split(proj, split_idx, axis=-1)

        qh = rotary(q.astype(jnp.float32).reshape(B, S, RET_HEADS, RET_QK_DIM), pos)
        kh = rotary(k.astype(jnp.float32).reshape(B, S, RET_HEADS, RET_QK_DIM), pos) * (RET_QK_DIM ** -0.5)
        vh = v.astype(jnp.float32).reshape(B, S, RET_HEADS, RET_V_DIM)
        o = head_groupnorm(retention_chunkwise(qh, kh, vh)).reshape(B, S, MIX_WIDTH).astype(dt)
        y_ret = (jax.nn.silu(g_ret) * o) @ w_ret_out[l].astype(dt)

        u = causal_dwconv(sc_c * sc_x, sc_conv_w[l])
        y_sc = (sc_b * u) @ w_sc_out[l].astype(dt)

        glu = cf_a * jax.nn.sigmoid(cf_b)
        c = causal_dwconv(glu, cf_conv_w[l], cf_conv_b[l])
        c = jax.nn.silu(layernorm(c, cf_ln_g[l], cf_ln_b[l]))
        y_cf = c @ w_cf_out[l].astype(dt)

        gates = jax.nn.sigmoid(gate_logits).reshape(B, S, N_BRANCH, D)
        merged = gates[:, :, 0] * y_ret + gates[:, :, 1] * y_sc + gates[:, :, 2] * y_cf
        x = x + rmsnorm(merged @ w_o[l].astype(dt), norm_mix_post[l])

        h = rmsnorm(x, norm_ffn_pre[l])
        up = causal_dwconv(h @ w_up[l].astype(dt), ffn_conv_w[l])
        gate, val = jnp.split(up, 2, axis=-1)
        f = (jax.nn.silu(gate) * val) @ w_down[l].astype(dt)
        x = x + rmsnorm(f, norm_ffn_post[l])
    return x
```

```python
import functools

import numpy as np
import jax
import jax.numpy as jnp
from jax import lax
from jax.experimental import pallas as pl
from jax.experimental.pallas import tpu as pltpu

RET_HEADS = 4
CF_KERNEL = 31
SC_KERNEL = 3
FFN_KERNEL = 3
ROPE_BASE = 10000.0
EPS = 1e-6

SEQ_TILE = 256
SUBLANES = 8
CF_TAIL = 32
CF_ROW_BLOCK = 32
FFN_COL_CHUNK = 256
VMEM_LIMIT_BYTES = 56 * 1024 * 1024

_BF16 = jnp.bfloat16
_F32 = jnp.float32


def _sigmoid(x):
    return 0.5 * jnp.tanh(0.5 * x) + 0.5


def _silu(x):
    return x * _sigmoid(x)


def _rms(x, g):
    ms = jnp.mean(x * x, axis=-1, keepdims=True)
    return x * lax.rsqrt(ms + EPS) * g


def _dot(a, b):
    return jnp.dot(a, b, preferred_element_type=_F32)


def _shift_rows(p, prev8, d):
    rolled = pltpu.roll(p, d, axis=0)
    row = lax.broadcasted_iota(jnp.int32, prev8.shape, 0)
    top = jnp.where(row < d, pltpu.roll(prev8, d, axis=0), rolled[0:SUBLANES])
    return jnp.concatenate([top, rolled[SUBLANES:]], axis=0)


def _conv3(p, prev8, w):
    return (w[2:3] * p + w[1:2] * _shift_rows(p, prev8, 1)
            + w[0:1] * _shift_rows(p, prev8, 2))


def _mixer_kernel(x_ref, gpre_ref, gpost_ref, win_ref, wret_ref, wsc_ref, wcf_ref, wo_ref,
                  scw_ref, cfw_ref, cfb_ref, lng_ref, lnb_ref, cos_ref, sin_ref,
                  decay_ref, zeta_ref, xi_ref, cd_ref, bd_ref,
                  out_ref, state_ref, sctail_ref, sh_ref, ccf_ref, *, dims):
    tm, d_model, mix, qk = dims
    dv = mix // RET_HEADS
    dk = qk // RET_HEADS
    s = pl.program_id(1)

    @pl.when(s == 0)
    def _():
        state_ref[...] = jnp.zeros_like(state_ref)
        sctail_ref[...] = jnp.zeros_like(sctail_ref)
        sh_ref[0, 0:CF_TAIL, :] = jnp.zeros((CF_TAIL, mix), _F32)

    x = x_ref[...]
    h = _rms(x, gpre_ref[...]).astype(_BF16)

    def proj(lo, n):
        return _dot(h, win_ref[:, lo:lo + n])

    o_q, o_k, o_v = 0, qk, 2 * qk
    o_gret = o_v + mix
    o_scb, o_scc, o_scx = o_gret + mix, o_gret + 2 * mix, o_gret + 3 * mix
    o_cfa, o_cfb = o_gret + 4 * mix, o_gret + 5 * mix
    o_gate = o_gret + 6 * mix

    lane = lax.broadcasted_iota(jnp.int32, (1, qk), 1)
    first_half = (lane % dk) < (dk // 2)
    cos = cos_ref[...]
    sin = sin_ref[...]

    def rotary(t):
        swapped = jnp.where(first_half, pltpu.roll(t, qk - dk // 2, axis=1),
                            pltpu.roll(t, dk // 2, axis=1))
        return t * cos + swapped * sin

    q = rotary(proj(o_q, qk))
    k = rotary(proj(o_k, qk)) * (dk ** -0.5)
    v = proj(o_v, mix)

    qb = q.astype(_BF16)
    vb = v.astype(_BF16)
    inner = []
    for hd in range(RET_HEADS):
        in_head = (lane >= hd * dk) & (lane < (hd + 1) * dk)
        kh = jnp.where(in_head, k, 0.0).astype(_BF16)
        scores = lax.dot_general(qb, kh, (((1,), (1,)), ((), ())),
                                 preferred_element_type=_F32)
        p = (scores * decay_ref[hd]).astype(_BF16)
        inner.append(_dot(p, vb[:, hd * dv:(hd + 1) * dv]))
    o_inner = jnp.concatenate(inner, axis=1)

    state = state_ref[...]
    o_cross = _dot(qb, state.astype(_BF16)) * xi_ref[...]
    kv = _dot(k.T.astype(_BF16), (v * zeta_ref[...]).astype(_BF16))
    state_ref[...] = cd_ref[...] * state + bd_ref[...] * kv

    o = o_inner + o_cross
    normed = []
    for hd in range(RET_HEADS):
        oh = o[:, hd * dv:(hd + 1) * dv]
        mu = jnp.mean(oh, axis=-1, keepdims=True)
        oc = oh - mu
        normed.append(oc * lax.rsqrt(jnp.mean(oc * oc, axis=-1, keepdims=True) + EPS))
    o_n = jnp.concatenate(normed, axis=1)
    a_ret = (_silu(proj(o_gret, mix)) * o_n).astype(_BF16)
    merged = _sigmoid(proj(o_gate, d_model)) * _dot(a_ret, wret_ref[...])

    pcx = proj(o_scc, mix) * proj(o_scx, mix)
    u = _conv3(pcx, sctail_ref[...], scw_ref[...])
    sctail_ref[...] = pcx[tm - SUBLANES:tm]
    a_sc = (proj(o_scb, mix) * u).astype(_BF16)
    merged = merged + _sigmoid(proj(o_gate + d_model, d_model)) * _dot(a_sc, wsc_ref[...])

    glu = proj(o_cfa, mix) * _sigmoid(proj(o_cfb, mix))
    sh_ref[0, CF_TAIL:CF_TAIL + tm, :] = glu
    hist = sh_ref[0]
    rows = CF_TAIL + tm
    for r in range(1, SUBLANES):
        sh_ref[r] = pltpu.roll(hist, rows - r, axis=0)
    for rb in range(tm // CF_ROW_BLOCK):
        r0 = rb * CF_ROW_BLOCK
        acc = jnp.broadcast_to(cfb_ref[...], (CF_ROW_BLOCK, mix))
        for tap in range(CF_KERNEL):
            a, r = divmod(CF_TAIL - (CF_KERNEL - 1) + tap, SUBLANES)
            acc = acc + cfw_ref[tap:tap + 1, :] * sh_ref[r, a * SUBLANES + r0:
                                                         a * SUBLANES + r0 + CF_ROW_BLOCK, :]
        mu = jnp.mean(acc, axis=-1, keepdims=True)
        cc = acc - mu
        y = cc * lax.rsqrt(jnp.mean(cc * cc, axis=-1, keepdims=True) + EPS)
        y = y * lng_ref[...] + lnb_ref[...]
        ccf_ref[r0:r0 + CF_ROW_BLOCK, :] = _silu(y).astype(_BF16)
    sh_ref[0, 0:CF_TAIL, :] = sh_ref[0, tm:tm + CF_TAIL, :]
    merged = merged + _sigmoid(proj(o_gate + 2 * d_model, d_model)) * _dot(ccf_ref[...], wcf_ref[...])

    z = _dot(merged.astype(_BF16), wo_ref[...])
    out_ref[...] = x + _rms(z, gpost_ref[...])


def _ffn_kernel(x_ref, gpre_ref, gpost_ref, wup_ref, cw_ref, wdn_ref, out_ref, tail_ref, *, dims):
    tm, d_ff = dims
    s = pl.program_id(1)

    @pl.when(s == 0)
    def _():
        tail_ref[...] = jnp.zeros_like(tail_ref)

    x = x_ref[...]
    h = _rms(x, gpre_ref[...]).astype(_BF16)

    def conv_up(lo):
        u = _dot(h, wup_ref[:, lo:lo + FFN_COL_CHUNK])
        prev8 = tail_ref[:, lo:lo + FFN_COL_CHUNK]
        tail_ref[:, lo:lo + FFN_COL_CHUNK] = u[tm - SUBLANES:tm]
        return _conv3(u, prev8, cw_ref[:, lo:lo + FFN_COL_CHUNK])

    f = None
    for j in range(d_ff // FFN_COL_CHUNK):
        lo = j * FFN_COL_CHUNK
        act = (_silu(conv_up(lo)) * conv_up(d_ff + lo)).astype(_BF16)
        part = _dot(act, wdn_ref[lo:lo + FFN_COL_CHUNK, :])
        f = part if f is None else f + part
    out_ref[...] = x + _rms(f, gpost_ref[...])


def _retention_tables(seq, tm, qk, mix):
    dk = qk // RET_HEADS
    dv = mix // RET_HEADS
    half = dk // 2
    inv = ROPE_BASE ** (-np.arange(half, dtype=np.float64) / half)
    ang = np.arange(seq, dtype=np.float64)[:, None] * inv[None, :]
    lane = np.arange(qk)
    idx = (lane % dk) % half
    sign = np.where((lane % dk) < half, -1.0, 1.0)
    cos_t = np.cos(ang)[:, idx]
    sin_t = np.sin(ang)[:, idx] * sign[None, :]

    gamma = 1.0 - np.exp2(-5.0 - np.arange(RET_HEADS, dtype=np.float64))
    i = np.arange(tm, dtype=np.float64)
    diff = i[:, None] - i[None, :]
    decay = np.where(diff[None] >= 0, gamma[:, None, None] ** np.maximum(diff, 0.0)[None], 0.0)
    zeta = gamma[None, :] ** (tm - 1.0 - i)[:, None]
    xi = gamma[None, :] ** (i + 1.0)[:, None]
    chunk_decay = gamma ** tm
    zeta_t = np.repeat(zeta, dv, axis=1)
    xi_t = np.repeat(xi, dv, axis=1)
    cd_t = np.repeat(chunk_decay, dv)[None, :]
    bd_t = (np.arange(qk)[:, None] // dk == np.arange(mix)[None, :] // dv).astype(np.float64)
    return tuple(jnp.asarray(t, dtype=_F32) for t in (cos_t, sin_t, decay, zeta_t, xi_t, cd_t, bd_t))


def _resident(shape):
    zeros = (0,) * len(shape)
    return pl.BlockSpec(shape, lambda b, s: zeros, pipeline_mode=pl.Buffered(1))


def _mixer_call(x, gpre, gpost, win, wret, wsc, wcf, wo, scw, cfw, cfb, lng, lnb, tables):
    batch, seq, d_model = x.shape
    mix = wret.shape[0]
    qk = mix // 2
    tm = SEQ_TILE
    cos_t, sin_t, decay, zeta_t, xi_t, cd_t, bd_t = tables
    tile = pl.BlockSpec((None, tm, d_model), lambda b, s: (b, s, 0))
    pos = pl.BlockSpec((tm, qk), lambda b, s: (s, 0))
    consts = (gpre, gpost, win, wret, wsc, wcf, wo, scw, cfw, cfb, lng, lnb)
    tabs = (decay, zeta_t, xi_t, cd_t, bd_t)
    return pl.pallas_call(
        functools.partial(_mixer_kernel, dims=(tm, d_model, mix, qk)),
        grid=(batch, seq // tm),
        in_specs=[tile] + [_resident(c.shape) for c in consts] + [pos, pos]
                 + [_resident(t.shape) for t in tabs],
        out_specs=tile,
        out_shape=jax.ShapeDtypeStruct(x.shape, x.dtype),
        scratch_shapes=[
            pltpu.VMEM((qk, mix), _F32),
            pltpu.VMEM((SUBLANES, mix), _F32),
            pltpu.VMEM((SUBLANES, CF_TAIL + tm, mix), _F32),
            pltpu.VMEM((tm, mix), _BF16),
        ],
        compiler_params=pltpu.CompilerParams(
            dimension_semantics=("arbitrary", "arbitrary"),
            vmem_limit_bytes=VMEM_LIMIT_BYTES),
        name="mixer",
    )(x, *consts, cos_t, sin_t, *tabs)


def _ffn_call(x, gpre, gpost, wup, cw, wdn):
    batch, seq, d_model = x.shape
    d_ff = wdn.shape[0]
    tm = SEQ_TILE
    tile = pl.BlockSpec((None, tm, d_model), lambda b, s: (b, s, 0))
    consts = (gpre, gpost, wup, cw, wdn)
    return pl.pallas_call(
        functools.partial(_ffn_kernel, dims=(tm, d_ff)),
        grid=(batch, seq // tm),
        in_specs=[tile] + [_resident(c.shape) for c in consts],
        out_specs=tile,
        out_shape=jax.ShapeDtypeStruct(x.shape, x.dtype),
        scratch_shapes=[pltpu.VMEM((SUBLANES, 2 * d_ff), _F32)],
        compiler_params=pltpu.CompilerParams(
            dimension_semantics=("arbitrary", "arbitrary"),
            vmem_limit_bytes=VMEM_LIMIT_BYTES),
        name="ffn",
    )(x, *consts)


def kernel(x, norm_mix_pre, norm_mix_post, norm_ffn_pre, norm_ffn_post, w_in, w_ret_out,
           sc_conv_w, w_sc_out, cf_conv_w, cf_conv_b, cf_ln_g, cf_ln_b, w_cf_out, w_o,
           w_up, ffn_conv_w, w_down):
    batch, seq, d_model = x.shape
    depth = w_in.shape[0]
    mix = w_ret_out.shape[1]
    d_ff = w_down.shape[1]
    assert seq % SEQ_TILE == 0 and d_ff % FFN_COL_CHUNK == 0
    assert cf_conv_w.shape[1] == CF_KERNEL and sc_conv_w.shape[1] == SC_KERNEL
    assert w_in.shape[2] == 8 * mix + 3 * d_model
    tables = _retention_tables(seq, SEQ_TILE, mix // 2, mix)
    row = lambda a: a.reshape(1, -1)
    for l in range(depth):
        x = _mixer_call(
            x, row(norm_mix_pre[l]), row(norm_mix_post[l]),
            w_in[l].astype(_BF16), w_ret_out[l].astype(_BF16), w_sc_out[l].astype(_BF16),
            w_cf_out[l].astype(_BF16), w_o[l].astype(_BF16),
            sc_conv_w[l], cf_conv_w[l], row(cf_conv_b[l]), row(cf_ln_g[l]), row(cf_ln_b[l]),
            tables)
        x = _ffn_call(x, row(norm_ffn_pre[l]), row(norm_ffn_post[l]),
                      w_up[l].astype(_BF16), ffn_conv_w[l], w_down[l].astype(_BF16))
    return x
```

```python
import functools

import numpy as np
import jax
import jax.numpy as jnp
from jax import lax
from jax.experimental import pallas as pl
from jax.experimental.pallas import tpu as pltpu

RET_HEADS = 4
CF_KERNEL = 31
SC_KERNEL = 3
FFN_KERNEL = 3
ROPE_BASE = 10000.0
EPS = 1e-6

SEQ_TILE = 256
SUBLANES = 8
LANES = 128
CF_TAIL = 32
CF_ROW_BLOCK = 32
GATE_CHUNK = 512
FFN_COL_CHUNK = 256
VMEM_LIMIT_BYTES = 56 * 1024 * 1024

_BF16 = jnp.bfloat16
_F32 = jnp.float32


def _sigmoid(x):
    return 0.5 * jnp.tanh(0.5 * x) + 0.5


def _silu(x):
    return x * _sigmoid(x)


def _rms(x, g):
    ms = jnp.mean(x * x, axis=-1, keepdims=True)
    return x * lax.rsqrt(ms + EPS) * g


def _dot(a, b):
    return jnp.dot(a, b, preferred_element_type=_F32)


def _conv3(p, hist_ref, first_block, w_ref, col0):
    tm = p.shape[0]
    outs = []
    for i in range(p.shape[1] // LANES):
        c = first_block + i
        blk = p[:, i * LANES:(i + 1) * LANES]
        hist_ref[c, SUBLANES:SUBLANES + tm, :] = blk
        lo = col0 + i * LANES
        outs.append(w_ref[2:3, lo:lo + LANES] * blk
                    + w_ref[1:2, lo:lo + LANES] * hist_ref[c, SUBLANES - 1:SUBLANES - 1 + tm, :]
                    + w_ref[0:1, lo:lo + LANES] * hist_ref[c, SUBLANES - 2:SUBLANES - 2 + tm, :])
        hist_ref[c, 0:SUBLANES, :] = blk[tm - SUBLANES:tm]
    return jnp.concatenate(outs, axis=1)


def _mixer_kernel(x_ref, gpre_ref, gpost_ref, win_ref, wret_ref, wsc_ref, wcf_ref, wo_ref,
                  scw_ref, cfw_ref, cfb_ref, lng_ref, lnb_ref, cos_ref, sin_ref,
                  decay_ref, zeta_ref, xi_ref, cd_ref, bd_ref,
                  out_ref, state_ref, schist_ref, cfhist_ref, ccf_ref, *, dims):
    tm, d_model, mix, qk = dims
    dv = mix // RET_HEADS
    dk = qk // RET_HEADS
    s = pl.program_id(1)

    @pl.when(s == 0)
    def _():
        state_ref[...] = jnp.zeros_like(state_ref)
        schist_ref[:, 0:SUBLANES, :] = jnp.zeros((mix // LANES, SUBLANES, LANES), _F32)
        cfhist_ref[:, 0:CF_TAIL, :] = jnp.zeros((mix // LANES, CF_TAIL, LANES), _F32)

    x = x_ref[...]
    h = _rms(x, gpre_ref[...]).astype(_BF16)

    def proj(lo, n):
        return _dot(h, win_ref[:, lo:lo + n])

    o_q, o_k, o_v = 0, qk, 2 * qk
    o_gret = o_v + mix
    o_scb, o_scc, o_scx = o_gret + mix, o_gret + 2 * mix, o_gret + 3 * mix
    o_cfa, o_cfb = o_gret + 4 * mix, o_gret + 5 * mix
    o_gate = o_gret + 6 * mix

    lane = lax.broadcasted_iota(jnp.int32, (1, qk), 1)
    first_half = (lane % dk) < (dk // 2)
    blocks = mix // LANES
    cf_first = CF_TAIL - (CF_KERNEL - 1)

    def rotary(t):
        swapped = jnp.where(first_half, pltpu.roll(t, qk - dk // 2, axis=1),
                            pltpu.roll(t, dk // 2, axis=1))
        return t * cos_ref[...] + swapped * sin_ref[...]

    def cf_rows(rb):
        r0 = rb * CF_ROW_BLOCK
        accs = []
        for c in range(blocks):
            acc = jnp.broadcast_to(cfb_ref[:, c * LANES:(c + 1) * LANES], (CF_ROW_BLOCK, LANES))
            for tap in range(CF_KERNEL):
                lo = cf_first + tap + r0
                acc = acc + (cfw_ref[tap:tap + 1, c * LANES:(c + 1) * LANES]
                             * cfhist_ref[c, lo:lo + CF_ROW_BLOCK, :])
            accs.append(acc)
        acc = jnp.concatenate(accs, axis=1)
        mu = jnp.mean(acc, axis=-1, keepdims=True)
        cc = acc - mu
        y = cc * lax.rsqrt(jnp.mean(cc * cc, axis=-1, keepdims=True) + EPS)
        y = y * lng_ref[...] + lnb_ref[...]
        ccf_ref[r0:r0 + CF_ROW_BLOCK, :] = _silu(y).astype(_BF16)

    def gate(i):
        return proj(o_gate + i * GATE_CHUNK, GATE_CHUNK)

    def cols(t, i):
        j = i % (d_model // GATE_CHUNK)
        return t[:, j * GATE_CHUNK:(j + 1) * GATE_CHUNK]

    q_raw = proj(o_q, qk)
    k_raw = proj(o_k, qk)
    cf_a = proj(o_cfa, mix)
    q = rotary(q_raw)
    k = rotary(k_raw) * (dk ** -0.5)
    qb = q.astype(_BF16)
    cf_b = proj(o_cfb, mix)
    khs = []
    for hd in range(RET_HEADS):
        in_head = (lane >= hd * dk) & (lane < (hd + 1) * dk)
        khs.append(jnp.where(in_head, k, 0.0).astype(_BF16))
    kt = k.T.astype(_BF16)
    v = proj(o_v, mix)
    glu = cf_a * _sigmoid(cf_b)
    for c in range(blocks):
        cfhist_ref[c, CF_TAIL:CF_TAIL + tm, :] = glu[:, c * LANES:(c + 1) * LANES]
    scores = [lax.dot_general(qb, khs[hd], (((1,), (1,)), ((), ())), preferred_element_type=_F32)
              for hd in range(RET_HEADS)]
    sc_c = proj(o_scc, mix)
    vb = v.astype(_BF16)
    vz = (v * zeta_ref[...]).astype(_BF16)
    cf_rows(0)
    sc_x = proj(o_scx, mix)
    ps = [(scores[hd] * decay_ref[hd]).astype(_BF16) for hd in range(RET_HEADS)]
    state = state_ref[...]
    o_inner = jnp.concatenate(
        [_dot(ps[hd], vb[:, hd * dv:(hd + 1) * dv]) for hd in range(RET_HEADS)], axis=1)
    o_cross = _dot(qb, state.astype(_BF16)) * xi_ref[...]
    kv = _dot(kt, vz)
    cf_rows(1)
    sc_b = proj(o_scb, mix)
    u = _conv3(sc_c * sc_x, schist_ref, 0, scw_ref, 0)
    g_ret = proj(o_gret, mix)
    state_ref[...] = cd_ref[...] * state + bd_ref[...] * kv
    o = o_inner + o_cross
    normed = []
    for hd in range(RET_HEADS):
        oh = o[:, hd * dv:(hd + 1) * dv]
        mu = jnp.mean(oh, axis=-1, keepdims=True)
        oc = oh - mu
        normed.append(oc * lax.rsqrt(jnp.mean(oc * oc, axis=-1, keepdims=True) + EPS))
    o_n = jnp.concatenate(normed, axis=1)
    a_sc = (sc_b * u).astype(_BF16)
    g0 = gate(0)
    a_ret = (_silu(g_ret) * o_n).astype(_BF16)
    cf_rows(2)
    y_sc = _dot(a_sc, wsc_ref[:, 0:d_model])
    cf_rows(3)
    g1 = gate(1)
    s0 = _sigmoid(g0)
    y_ret = _dot(a_ret, wret_ref[:, 0:d_model])
    s1 = _sigmoid(g1)
    cf_rows(4)
    g2 = gate(2)
    m0 = s0 * cols(y_ret, 0)
    m1 = s1 * cols(y_ret, 1)
    cf_rows(5)
    g3 = gate(3)
    m0 = m0 + _sigmoid(g2) * cols(y_sc, 2)
    cf_rows(6)
    g4 = gate(4)
    m1 = m1 + _sigmoid(g3) * cols(y_sc, 3)
    cf_rows(7)
    g5 = gate(5)
    s4 = _sigmoid(g4)
    for c in range(blocks):
        cfhist_ref[c, 0:CF_TAIL, :] = cfhist_ref[c, tm:tm + CF_TAIL, :]
    y_cf = _dot(ccf_ref[...], wcf_ref[:, 0:d_model])
    s5 = _sigmoid(g5)
    merged = jnp.concatenate([m0 + s4 * cols(y_cf, 4), m1 + s5 * cols(y_cf, 5)], axis=1)
    z = _dot(merged.astype(_BF16), wo_ref[:, 0:d_model])
    out_ref[...] = x + _rms(z, gpost_ref[...])


def _ffn_kernel(x_ref, gpre_ref, gpost_ref, wup_ref, cw_ref, wdn_ref, out_ref, hist_ref, *, dims):
    tm, d_ff = dims
    d_model = x_ref.shape[-1]
    s = pl.program_id(1)

    @pl.when(s == 0)
    def _():
        hist_ref[:, 0:SUBLANES, :] = jnp.zeros((2 * d_ff // LANES, SUBLANES, LANES), _F32)

    x = x_ref[...]
    h = _rms(x, gpre_ref[...]).astype(_BF16)

    def up(lo):
        return _dot(h, wup_ref[:, lo:lo + FFN_COL_CHUNK])

    chunks = d_ff // FFN_COL_CHUNK
    f = None
    nxt = (up(0), up(d_ff))
    for j in range(chunks):
        lo = j * FFN_COL_CHUNK
        gate_u, val_u = nxt
        if j + 1 < chunks:
            nxt = (up(lo + FFN_COL_CHUNK), up(d_ff + lo + FFN_COL_CHUNK))
        gate = _conv3(gate_u, hist_ref, lo // LANES, cw_ref, lo)
        val = _conv3(val_u, hist_ref, (d_ff + lo) // LANES, cw_ref, d_ff + lo)
        act = (_silu(gate) * val).astype(_BF16)
        part = _dot(act, wdn_ref[lo:lo + FFN_COL_CHUNK, 0:d_model])
        f = part if f is None else f + part
    out_ref[...] = x + _rms(f, gpost_ref[...])


def _retention_tables(seq, tm, qk, mix):
    dk = qk // RET_HEADS
    dv = mix // RET_HEADS
    half = dk // 2
    inv = ROPE_BASE ** (-np.arange(half, dtype=np.float64) / half)
    ang = np.arange(seq, dtype=np.float64)[:, None] * inv[None, :]
    lane = np.arange(qk)
    idx = (lane % dk) % half
    sign = np.where((lane % dk) < half, -1.0, 1.0)
    cos_t = np.cos(ang)[:, idx]
    sin_t = np.sin(ang)[:, idx] * sign[None, :]

    gamma = 1.0 - np.exp2(-5.0 - np.arange(RET_HEADS, dtype=np.float64))
    i = np.arange(tm, dtype=np.float64)
    diff = i[:, None] - i[None, :]
    decay = np.where(diff[None] >= 0, gamma[:, None, None] ** np.maximum(diff, 0.0)[None], 0.0)
    zeta = gamma[None, :] ** (tm - 1.0 - i)[:, None]
    xi = gamma[None, :] ** (i + 1.0)[:, None]
    chunk_decay = gamma ** tm
    zeta_t = np.repeat(zeta, dv, axis=1)
    xi_t = np.repeat(xi, dv, axis=1)
    cd_t = np.repeat(chunk_decay, dv)[None, :]
    bd_t = (np.arange(qk)[:, None] // dk == np.arange(mix)[None, :] // dv).astype(np.float64)
    return tuple(jnp.asarray(t, dtype=_F32) for t in (cos_t, sin_t, decay, zeta_t, xi_t, cd_t, bd_t))


def _matmul_weight(w):
    assert (w.shape[1] // LANES) % 8 == 0
    return jnp.pad(w.astype(_BF16), ((0, 0), (0, LANES)))


def _resident(shape):
    zeros = (0,) * len(shape)
    return pl.BlockSpec(shape, lambda b, s: zeros, pipeline_mode=pl.Buffered(1))


def _mixer_call(x, gpre, gpost, win, wret, wsc, wcf, wo, scw, cfw, cfb, lng, lnb, tables):
    batch, seq, d_model = x.shape
    mix = wret.shape[0]
    qk = mix // 2
    tm = SEQ_TILE
    assert tm // CF_ROW_BLOCK == 8 and 3 * d_model // GATE_CHUNK == 6
    cos_t, sin_t, decay, zeta_t, xi_t, cd_t, bd_t = tables
    tile = pl.BlockSpec((None, tm, d_model), lambda b, s: (b, s, 0))
    pos = pl.BlockSpec((tm, qk), lambda b, s: (s, 0))
    consts = (gpre, gpost, win, wret, wsc, wcf, wo, scw, cfw, cfb, lng, lnb)
    tabs = (decay, zeta_t, xi_t, cd_t, bd_t)
    return pl.pallas_call(
        functools.partial(_mixer_kernel, dims=(tm, d_model, mix, qk)),
        grid=(batch, seq // tm),
        in_specs=[tile] + [_resident(c.shape) for c in consts] + [pos, pos]
                 + [_resident(t.shape) for t in tabs],
        out_specs=tile,
        out_shape=jax.ShapeDtypeStruct(x.shape, x.dtype),
        scratch_shapes=[
            pltpu.VMEM((qk, mix), _F32),
            pltpu.VMEM((mix // LANES, SUBLANES + tm, LANES), _F32),
            pltpu.VMEM((mix // LANES, CF_TAIL + tm, LANES), _F32),
            pltpu.VMEM((tm, mix), _BF16),
        ],
        compiler_params=pltpu.CompilerParams(
            dimension_semantics=("arbitrary", "arbitrary"),
            vmem_limit_bytes=VMEM_LIMIT_BYTES),
        name="mixer",
    )(x, *consts, cos_t, sin_t, *tabs)


def _ffn_call(x, gpre, gpost, wup, cw, wdn):
    batch, seq, d_model = x.shape
    d_ff = wdn.shape[0]
    tm = SEQ_TILE
    tile = pl.BlockSpec((None, tm, d_model), lambda b, s: (b, s, 0))
    consts = (gpre, gpost, wup, cw, wdn)
    return pl.pallas_call(
        functools.partial(_ffn_kernel, dims=(tm, d_ff)),
        grid=(batch, seq // tm),
        in_specs=[tile] + [_resident(c.shape) for c in consts],
        out_specs=tile,
        out_shape=jax.ShapeDtypeStruct(x.shape, x.dtype),
        scratch_shapes=[pltpu.VMEM((2 * d_ff // LANES, SUBLANES + tm, LANES), _F32)],
        compiler_params=pltpu.CompilerParams(
            dimension_semantics=("arbitrary", "arbitrary"),
            vmem_limit_bytes=VMEM_LIMIT_BYTES),
        name="ffn",
    )(x, *consts)


def kernel(x, norm_mix_pre, norm_mix_post, norm_ffn_pre, norm_ffn_post, w_in, w_ret_out,
           sc_conv_w, w_sc_out, cf_conv_w, cf_conv_b, cf_ln_g, cf_ln_b, w_cf_out, w_o,
           w_up, ffn_conv_w, w_down):
    batch, seq, d_model = x.shape
    depth = w_in.shape[0]
    mix = w_ret_out.shape[1]
    d_ff = w_down.shape[1]
    assert seq % SEQ_TILE == 0 and d_ff % FFN_COL_CHUNK == 0
    assert cf_conv_w.shape[1] == CF_KERNEL and sc_conv_w.shape[1] == SC_KERNEL
    assert w_in.shape[2] == 8 * mix + 3 * d_model
    tables = _retention_tables(seq, SEQ_TILE, mix // 2, mix)
    row = lambda a: a.reshape(1, -1)
    for l in range(depth):
        x = _mixer_call(
            x, row(norm_mix_pre[l]), row(norm_mix_post[l]),
            _matmul_weight(w_in[l]), _matmul_weight(w_ret_out[l]), _matmul_weight(w_sc_out[l]),
            _matmul_weight(w_cf_out[l]), _matmul_weight(w_o[l]),
            sc_conv_w[l], cf_conv_w[l], row(cf_conv_b[l]), row(cf_ln_g[l]), row(cf_ln_b[l]),
            tables)
        x = _ffn_call(x, row(norm_ffn_pre[l]), row(norm_ffn_post[l]),
                      w_up[l].astype(_BF16), ffn_conv_w[l], _matmul_weight(w_down[l]))
    return x
```
